```python
import jax, jax.numpy as jnp
from jax import lax
import numpy as np

D_MODEL = 2048
BATCH = 4
SEQ = 4096
DEPTH = 2

N_MIXERS = 2
EPS = 1e-6
CONV_WIDTH = 4
SSD_EXPAND = 2
SSD_D_INNER = SSD_EXPAND * D_MODEL
SSD_HEAD_DIM = 64
SSD_HEADS = SSD_D_INNER // SSD_HEAD_DIM
SSD_GROUPS = 8
SSD_HEADS_PER_GROUP = SSD_HEADS // SSD_GROUPS
SSD_D_STATE = 128
SSD_CHUNK = 128
SSD_BC_WIDTH = SSD_GROUPS * SSD_D_STATE
SSD_CONV_DIM = SSD_D_INNER + 2 * SSD_BC_WIDTH
SSD_PROJ = SSD_D_INNER + SSD_CONV_DIM + SSD_HEADS
ML_D_INNER = 2 * D_MODEL
ML_HEADS = 8
ML_V_DIM = ML_D_INNER // ML_HEADS
ML_QK_DIM = ML_V_DIM // 2
ML_QK_WIDTH = 2 * ML_HEADS * ML_QK_DIM
ML_CHUNK = 64
ML_PROJ = ML_QK_WIDTH + 3 * ML_D_INNER + 2 * ML_HEADS
N_SSD = (DEPTH + 1) // 2
N_ML = DEPTH // 2

kernel_name = "hybrid_ssd_mlstm_adaln_trunk"

F32 = jnp.float32


def rmsnorm(u, w):
    u32 = u.astype(F32)
    y = u32 * lax.rsqrt(jnp.mean(u32 * u32, axis=-1, keepdims=True) + EPS) * w.astype(F32)
    return y.astype(u.dtype)


def causal_conv(u, w, b):
    ch = u.shape[-1]
    y = lax.conv_general_dilated(
        u, w[:, None, :].astype(u.dtype), window_strides=(1,),
        padding=[(CONV_WIDTH - 1, 0)], dimension_numbers=("NWC", "WIO", "NWC"),
        feature_group_count=ch)
    return y + b.astype(u.dtype)


def to_chunks(t, chunk):
    b, n = t.shape[0], t.shape[1]
    t = t.reshape((b, n // chunk, chunk) + t.shape[2:])
    return jnp.moveaxis(t, 1, 0)


def from_chunks(t):
    t = jnp.moveaxis(t, 0, 1)
    return t.reshape((t.shape[0], -1) + t.shape[3:])


def ssd_chunked_scan(xdt, a, bmat, cmat):
    bsz = xdt.shape[0]
    causal = jnp.tril(jnp.ones((SSD_CHUNK, SSD_CHUNK), dtype=bool))[None, :, :, None, None]

    def step(state, inp):
        xc, ac, bc, cc = inp
        acs = jnp.cumsum(ac, axis=1)
        seg = acs[:, :, None] - acs[:, None, :]
        decay = jnp.exp(jnp.where(causal, seg, -jnp.inf))
        cb = jnp.einsum("btgn,bsgn->btsg", cc, bc)
        y_diag = jnp.einsum("btsgr,bsgrp->btgrp", cb[..., None] * decay, xc)
        y_off = jnp.einsum("btgn,bgrpn->btgrp", cc, state) * jnp.exp(acs)[..., None]
        total = acs[:, -1]
        w_end = jnp.exp(total[:, None] - acs)
        new_state = state * jnp.exp(total)[..., None, None] + jnp.einsum(
            "bsgn,bsgr,bsgrp->bgrpn", bc, w_end, xc)
        return new_state, y_diag + y_off

    state0 = jnp.zeros((bsz, SSD_GROUPS, SSD_HEADS_PER_GROUP, SSD_HEAD_DIM, SSD_D_STATE), F32)
    _, y = lax.scan(step, state0, (to_chunks(xdt, SSD_CHUNK), to_chunks(a, SSD_CHUNK),
                                   to_chunks(bmat, SSD_CHUNK), to_chunks(cmat, SSD_CHUNK)))
    return from_chunks(y)


def ssd_mixer(h, w_in, conv_w, conv_b, dt_bias, a_log, d_skip, norm_w, w_out):
    bsz, seqlen, _ = h.shape
    proj = h @ w_in
    z, xbc, dt = jnp.split(proj, [SSD_D_INNER, SSD_D_INNER + SSD_CONV_DIM], axis=-1)
    xbc = jax.nn.silu(causal_conv(xbc, conv_w, conv_b)).astype(F32)
    xs, bmat, cmat = jnp.split(xbc, [SSD_D_INNER, SSD_D_INNER + SSD_BC_WIDTH], axis=-1)
    xs = xs.reshape(bsz, seqlen, SSD_GROUPS, SSD_HEADS_PER_GROUP, SSD_HEAD_DIM)
    bmat = bmat.reshape(bsz, seqlen, SSD_GROUPS, SSD_D_STATE)
    cmat = cmat.reshape(bsz, seqlen, SSD_GROUPS, SSD_D_STATE)
    dt = jax.nn.softplus(dt.astype(F32) + dt_bias.astype(F32))
    dt = dt.reshape(bsz, seqlen, SSD_GROUPS, SSD_HEADS_PER_GROUP)
    a = -jnp.exp(a_log.astype(F32)).reshape(SSD_GROUPS, SSD_HEADS_PER_GROUP) * dt
    y = ssd_chunked_scan(xs * dt[..., None], a, bmat, cmat)
    y = y + d_skip.astype(F32).reshape(SSD_GROUPS, SSD_HEADS_PER_GROUP, 1) * xs
    g = (y.reshape(bsz, seqlen, SSD_D_INNER) * jax.nn.silu(z.astype(F32)))
    g = g.reshape(bsz, seqlen, SSD_GROUPS, SSD_D_INNER // SSD_GROUPS)
    g = g * lax.rsqrt(jnp.mean(g * g, axis=-1, keepdims=True) + EPS)
    g = g.reshape(bsz, seqlen, SSD_D_INNER) * norm_w.astype(F32)
    return g.astype(h.dtype) @ w_out


def mlstm_chunked(q, k, v, i_pre, log_f):
    bsz = q.shape[0]
    causal = jnp.tril(jnp.ones((ML_CHUNK, ML_CHUNK), dtype=bool))[None, :, :, None]

    def step(carry, inp):
        c_st, n_st, m_st = carry
        qc, kc, vc, ic, fc = inp
        bcum = jnp.cumsum(fc, axis=1)
        dlog = bcum[:, :, None] - bcum[:, None, :] + ic[:, None, :]
        dlog = jnp.where(causal, dlog, -jnp.inf)
        inter = bcum + m_st[:, None]
        m_t = jnp.maximum(inter, jnp.max(dlog, axis=2))
        s = jnp.einsum("bthk,bshk->btsh", qc, kc) * jnp.exp(dlog - m_t[:, :, None])
        inter_w = jnp.exp(inter - m_t)
        num = jnp.einsum("btsh,bshv->bthv", s, vc) + inter_w[..., None] * jnp.einsum(
            "bthk,bhkv->bthv", qc, c_st)
        den = jnp.sum(s, axis=2) + inter_w * jnp.einsum("bthk,bhk->bth", qc, n_st)
        h_out = num / jnp.maximum(jnp.abs(den), jnp.exp(-m_t))[..., None]
        total = bcum[:, -1]
        wlog = total[:, None] - bcum + ic
        m_new = jnp.maximum(total + m_st, jnp.max(wlog, axis=1))
        w = jnp.exp(wlog - m_new[:, None])
        carry_w = jnp.exp(total + m_st - m_new)
        c_new = carry_w[..., None, None] * c_st + jnp.einsum("bsh,bshk,bshv->bhkv", w, kc, vc)
        n_new = carry_w[..., None] * n_st + jnp.einsum("bsh,bshk->bhk", w, kc)
        return (c_new, n_new, m_new), h_out

    carry0 = (jnp.zeros((bsz, ML_HEADS, ML_QK_DIM, ML_V_DIM), F32),
              jnp.zeros((bsz, ML_HEADS, ML_QK_DIM), F32),
              jnp.zeros((bsz, ML_HEADS), F32))
    _, h_out = lax.scan(step, carry0, (to_chunks(q, ML_CHUNK), to_chunks(k, ML_CHUNK),
                                       to_chunks(v, ML_CHUNK), to_chunks(i_pre, ML_CHUNK),
                                       to_chunks(log_f, ML_CHUNK)))
    return from_chunks(h_out)


def mlstm_mixer(h, w_in, conv_w, conv_b, igate_b, fgate_b, head_norm_w, w_out):
    bsz, seqlen, _ = h.shape
    proj = h @ w_in
    qk, v, o, z, gates = jnp.split(
        proj, [ML_QK_WIDTH, ML_QK_WIDTH + ML_D_INNER, ML_QK_WIDTH + 2 * ML_D_INNER,
               ML_QK_WIDTH + 3 * ML_D_INNER], axis=-1)
    qk = jax.nn.silu(causal_conv(qk, conv_w, conv_b)).astype(F32)
    q, k = jnp.split(qk, 2, axis=-1)
    q = q.reshape(bsz, seqlen, ML_HEADS, ML_QK_DIM)
    k = k.reshape(bsz, seqlen, ML_HEADS, ML_QK_DIM) * (ML_QK_DIM ** -0.5)
    v = v.astype(F32).reshape(bsz, seqlen, ML_HEADS, ML_V_DIM)
    i_pre, f_pre = jnp.split(gates.astype(F32), 2, axis=-1)
    i_pre = i_pre + igate_b.astype(F32)
    log_f = jax.nn.log_sigmoid(f_pre + fgate_b.astype(F32))
    h_tilde = mlstm_chunked(q, k, v, i_pre, log_f)
    o_gate = jax.nn.sigmoid(o.astype(F32)).reshape(bsz, seqlen, ML_HEADS, ML_V_DIM)
    hc = o_gate * h_tilde
    hc = hc * lax.rsqrt(jnp.mean(hc * hc, axis=-1, keepdims=True) + EPS)
    hc = hc * head_norm_w.astype(F32).reshape(ML_HEADS, ML_V_DIM)
    y = hc.reshape(bsz, seqlen, ML_D_INNER) * jax.nn.silu(z.astype(F32))
    return y.astype(h.dtype) @ w_out


def setup_inputs(seed: int = 0) -> dict:
    key = jax.random.key(seed)
    ks = jax.random.split(key, 24)
    nrm = jax.random.normal
    x = nrm(ks[0], (BATCH, SEQ, D_MODEL), F32)
    c = nrm(ks[1], (BATCH, D_MODEL), F32)
    norm_w = 1.0 + 0.02 * nrm(ks[2], (DEPTH, D_MODEL), F32)
    ada_w = 0.5 * D_MODEL ** -0.5 * nrm(ks[3], (DEPTH, D_MODEL, 3 * D_MODEL), F32)
    ada_b = 0.02 * nrm(ks[4], (DEPTH, 3 * D_MODEL), F32)
    ssd_w_in = D_MODEL ** -0.5 * nrm(ks[5], (N_SSD, D_MODEL, SSD_PROJ), F32)
    ssd_conv_w = CONV_WIDTH ** -0.5 * nrm(ks[6], (N_SSD, CONV_WIDTH, SSD_CONV_DIM), F32)
    ssd_conv_b = 0.02 * nrm(ks[7], (N_SSD, SSD_CONV_DIM), F32)
    u = jax.random.uniform(ks[8], (N_SSD, SSD_HEADS), F32)
    dt0 = jnp.exp(u * (np.log(0.1) - np.log(1e-3)).astype(np.float32) + np.float32(np.log(1e-3)))
    ssd_dt_bias = dt0 + jnp.log(-jnp.expm1(-dt0))
    ssd_a_log = jnp.log(jax.random.uniform(ks[9], (N_SSD, SSD_HEADS), F32, 1.0, 16.0))
    ssd_d = 1.0 + 0.02 * nrm(ks[10], (N_SSD, SSD_HEADS), F32)
    ssd_norm_w = 1.0 + 0.02 * nrm(ks[11], (N_SSD, SSD_D_INNER), F32)
    ssd_w_out = SSD_D_INNER ** -0.5 * nrm(ks[12], (N_SSD, SSD_D_INNER, D_MODEL), F32)
    ml_w_in = D_MODEL ** -0.5 * nrm(ks[13], (N_ML, D_MODEL, ML_PROJ), F32)
    ml_conv_w = CONV_WIDTH ** -0.5 * nrm(ks[14], (N_ML, CONV_WIDTH, ML_QK_WIDTH), F32)
    ml_conv_b = 0.02 * nrm(ks[15], (N_ML, ML_QK_WIDTH), F32)
    ml_igate_b = 0.1 * nrm(ks[16], (N_ML, ML_HEADS), F32)
    ml_fgate_b = 3.0 + 3.0 * jax.random.uniform(ks[17], (N_ML, ML_HEADS), F32)
    ml_norm_w = 1.0 + 0.02 * nrm(ks[18], (N_ML, ML_D_INNER), F32)
    ml_w_out = ML_D_INNER ** -0.5 * nrm(ks[19], (N_ML, ML_D_INNER, D_MODEL), F32)
    final_norm_w = 1.0 + 0.02 * nrm(ks[20], (D_MODEL,), F32)
    return {"x": x, "c": c, "norm_w": norm_w, "ada_w": ada_w, "ada_b": ada_b,
            "ssd_w_in": ssd_w_in, "ssd_conv_w": ssd_conv_w, "ssd_conv_b": ssd_conv_b,
            "ssd_dt_bias": ssd_dt_bias, "ssd_a_log": ssd_a_log, "ssd_d": ssd_d,
            "ssd_norm_w": ssd_norm_w, "ssd_w_out": ssd_w_out,
            "ml_w_in": ml_w_in, "ml_conv_w": ml_conv_w, "ml_conv_b": ml_conv_b,
            "ml_igate_b": ml_igate_b, "ml_fgate_b": ml_fgate_b, "ml_norm_w": ml_norm_w,
            "ml_w_out": ml_w_out, "final_norm_w": final_norm_w}


def reference(x, c, norm_w, ada_w, ada_b, ssd_w_in, ssd_conv_w, ssd_conv_b, ssd_dt_bias,
              ssd_a_log, ssd_d, ssd_norm_w, ssd_w_out, ml_w_in, ml_conv_w, ml_conv_b,
              ml_igate_b, ml_fgate_b, ml_norm_w, ml_w_out, final_norm_w):
    cond = jax.nn.silu(c)
    for i in range(DEPTH):
        mod = cond @ ada_w[i] + ada_b[i]
        shift, scale, gate = jnp.split(mod[:, None, :], 3, axis=-1)
        h = rmsnorm(x, norm_w[i]) * (1.0 + scale) + shift
        j = i // N_MIXERS
        if i % N_MIXERS == 0:
            out = ssd_mixer(h, ssd_w_in[j], ssd_conv_w[j], ssd_conv_b[j], ssd_dt_bias[j],
                            ssd_a_log[j], ssd_d[j], ssd_norm_w[j], ssd_w_out[j])
        else:
            out = mlstm_mixer(h, ml_w_in[j], ml_conv_w[j], ml_conv_b[j], ml_igate_b[j],
                              ml_fgate_b[j], ml_norm_w[j], ml_w_out[j])
        x = x + gate * out
    return rmsnorm(x, final_norm_w)
```

```python
import functools

import jax
import jax.numpy as jnp
from jax import lax
from jax.experimental import pallas as pl
from jax.experimental.pallas import tpu as pltpu

F32 = jnp.float32
BF16 = jnp.bfloat16

V7X_SUBLANES = 8
V7X_LANES = 128
V7X_VMEM_LIMIT_BYTES = 56 * 1024 * 1024

EPS = 1e-6
CONV_WIDTH = 4
D_MODEL = 2048
SSD_D_INNER = 2 * D_MODEL
SSD_HEAD_DIM = 64
SSD_HEADS = SSD_D_INNER // SSD_HEAD_DIM
SSD_GROUPS = 8
SSD_HEADS_PER_GROUP = SSD_HEADS // SSD_GROUPS
SSD_D_STATE = 128
SSD_CHUNK = 128
SSD_BC_WIDTH = SSD_GROUPS * SSD_D_STATE
SSD_GROUP_WIDTH = SSD_D_INNER // SSD_GROUPS
SSD_PAIRS_PER_GROUP = SSD_GROUP_WIDTH // V7X_LANES
SSD_WIDE = SSD_D_INNER + SSD_D_INNER + 2 * SSD_BC_WIDTH
ML_D_INNER = 2 * D_MODEL
ML_HEADS = 8
ML_V_DIM = ML_D_INNER // ML_HEADS
ML_QK_DIM = ML_V_DIM // 2
ML_QK_WIDTH = 2 * ML_HEADS * ML_QK_DIM
ML_WIDE = ML_QK_WIDTH + 3 * ML_D_INNER
ML_CHUNK = 256
ML_VEXT = ML_V_DIM + V7X_LANES


def _sigmoid(v):
    return 1.0 / (1.0 + jnp.exp(-v))


def _silu(v):
    return v * _sigmoid(v)


def _ada_kernel(c_ref, w_ref, b_ref, o_ref):
    cond = _silu(c_ref[...]).astype(BF16)
    o_ref[0] = jnp.dot(cond, w_ref[0].astype(BF16), preferred_element_type=F32) + b_ref[0]


def _ada_modulation(c, ada_w, ada_b):
    depth, d, n = ada_w.shape
    bsz = c.shape[0]
    rows = -(-bsz // V7X_SUBLANES) * V7X_SUBLANES
    c_pad = jnp.zeros((rows, d), F32).at[:bsz].set(c)
    tn = 768
    out = pl.pallas_call(
        _ada_kernel,
        grid=(depth, n // tn),
        in_specs=[
            pl.BlockSpec((rows, d), lambda i, j: (0, 0)),
            pl.BlockSpec((1, d, tn), lambda i, j: (i, 0, j)),
            pl.BlockSpec((1, 1, tn), lambda i, j: (i, 0, j)),
        ],
        out_specs=pl.BlockSpec((1, rows, tn), lambda i, j: (i, 0, j)),
        out_shape=jax.ShapeDtypeStruct((depth, rows, n), F32),
        compiler_params=pltpu.CompilerParams(
            dimension_semantics=("arbitrary", "arbitrary"), vmem_limit_bytes=V7X_VMEM_LIMIT_BYTES),
        name="ada_mod",
    )(c_pad, ada_w, ada_b.reshape(depth, 1, n))
    return out[:, :bsz]


def _proj_kernel(x_ref, nw_ref, scale_ref, shift_ref, w_ref, ws_ref, o_ref, os_ref, h_scr, *, small_transposed):
    @pl.when(pl.program_id(1) == 0)
    def _():
        x = x_ref[...]
        ms = jnp.mean(x * x, axis=-1, keepdims=True)
        y = x * lax.rsqrt(ms + EPS) * nw_ref[...]
        hb = (y * (1.0 + scale_ref[0]) + shift_ref[0]).astype(BF16)
        h_scr[...] = hb
        if small_transposed:
            os_ref[...] = lax.dot_general(ws_ref[...], hb, (((1,), (1,)), ((), ())), preferred_element_type=F32)
        else:
            os_ref[...] = jnp.dot(hb, ws_ref[...], preferred_element_type=F32)

    o_ref[...] = jnp.dot(h_scr[...], w_ref[...], preferred_element_type=F32).astype(o_ref.dtype)


def _norm_mod_proj(x2, norm_w, scale, shift, w_wide, w_small, seqlen, *, small_transposed):
    bt, d = x2.shape
    n = w_wide.shape[1]
    tm, tn = min(1024, seqlen), 1024
    tiles_per_seq = seqlen // tm
    if small_transposed:
        small_spec = pl.BlockSpec((V7X_LANES, tm), lambda i, j: (0, i))
        small_shape = jax.ShapeDtypeStruct((V7X_LANES, bt), F32)
        ws_spec = pl.BlockSpec((V7X_LANES, d), lambda i, j: (0, 0))
    else:
        small_spec = pl.BlockSpec((tm, V7X_LANES), lambda i, j: (i, 0))
        small_shape = jax.ShapeDtypeStruct((bt, V7X_LANES), F32)
        ws_spec = pl.BlockSpec((d, V7X_LANES), lambda i, j: (0, 0))
    return pl.pallas_call(
        functools.partial(_proj_kernel, small_transposed=small_transposed),
        grid=(bt // tm, n // tn),
        in_specs=[
            pl.BlockSpec((tm, d), lambda i, j: (i, 0)),
            pl.BlockSpec((1, d), lambda i, j: (0, 0)),
            pl.BlockSpec((1, 1, d), lambda i, j: (i // tiles_per_seq, 0, 0)),
            pl.BlockSpec((1, 1, d), lambda i, j: (i // tiles_per_seq, 0, 0)),
            pl.BlockSpec((d, tn), lambda i, j: (0, j)),
            ws_spec,
        ],
        out_specs=[pl.BlockSpec((tm, tn), lambda i, j: (i, j)), small_spec],
        out_shape=(jax.ShapeDtypeStruct((bt, n), BF16), small_shape),
        scratch_shapes=[pltpu.VMEM((tm, d), BF16)],
        compiler_params=pltpu.CompilerParams(
            dimension_semantics=("arbitrary", "arbitrary"), vmem_limit_bytes=V7X_VMEM_LIMIT_BYTES),
        name="norm_mod_proj",
    )(x2, norm_w.reshape(1, d), scale, shift, w_wide, w_small)


def _causal_conv_silu(u_lowp, tail, w, b):
    u = u_lowp.astype(F32)
    rows, cols = u.shape
    row = lax.broadcasted_iota(jnp.int32, (V7X_SUBLANES, cols), 0)
    acc = u * w[CONV_WIDTH - 1:CONV_WIDTH, :] + b
    for s in range(1, CONV_WIDTH):
        rolled = pltpu.roll(u, s, axis=0)
        rolled_tail = pltpu.roll(tail, s, axis=0)
        first = jnp.where(row < s, rolled_tail, rolled[0:V7X_SUBLANES])
        shifted = jnp.concatenate([first, rolled[V7X_SUBLANES:]], axis=0)
        acc = acc + shifted * w[CONV_WIDTH - 1 - s:CONV_WIDTH - s, :]
    return _silu(acc), u[rows - V7X_SUBLANES:rows]


def _cumsum_lanes(v):
    width = v.shape[-1]
    lane = lax.broadcasted_iota(jnp.int32, v.shape, v.ndim - 1)
    s = 1
    while s < width:
        v = v + jnp.where(lane >= s, pltpu.roll(v, s, axis=v.ndim - 1), 0.0)
        s *= 2
    return v


def _ssd_kernel(z_ref, x_ref, bc_ref, dt_ref, cwx_ref, cbx_ref, cwbc_ref, cbbc_ref, dtb_ref, alog_ref,
                dskip_ref, nw_ref, o_ref,
                halo_x, halo_bc, xs_scr, b_scr, c_scr, y_scr, st_scr, acs_scr, acst_scr, dtt_scr):
    L = SSD_CHUNK
    gw = SSD_GROUP_WIDTH
    ns = SSD_D_STATE

    @pl.when(pl.program_id(1) == 0)
    def _():
        halo_x[...] = jnp.zeros_like(halo_x)
        halo_bc[...] = jnp.zeros_like(halo_bc)
        st_scr[...] = jnp.zeros_like(st_scr)

    for g in range(SSD_GROUPS):
        sl = slice(g * gw, (g + 1) * gw)
        act, tail = _causal_conv_silu(x_ref[:, sl], halo_x[:, sl], cwx_ref[:, sl], cbx_ref[:, sl])
        xs_scr[g] = act
        halo_x[:, sl] = tail
    for g in range(2 * SSD_GROUPS):
        sl = slice(g * ns, (g + 1) * ns)
        act, tail = _causal_conv_silu(bc_ref[:, sl], halo_bc[:, sl], cwbc_ref[:, sl], cbbc_ref[:, sl])
        if g < SSD_GROUPS:
            b_scr[g] = act
        else:
            c_scr[g - SSD_GROUPS] = act
        halo_bc[:, sl] = tail

    dt = jnp.logaddexp(dt_ref[...] + dtb_ref[...], 0.0)
    a = -jnp.exp(alog_ref[...]) * dt
    acs_t = _cumsum_lanes(a.T)
    acst_scr[...] = acs_t
    dtt_scr[...] = dt.T
    acs_scr[...] = acs_t.T

    t_idx = lax.broadcasted_iota(jnp.int32, (L, L), 0)
    s_idx = lax.broadcasted_iota(jnp.int32, (L, L), 1)
    causal = s_idx <= t_idx
    lane = lax.broadcasted_iota(jnp.int32, (L, V7X_LANES), 1)
    low_half = lane < SSD_HEAD_DIM
    low_half_row = lax.broadcasted_iota(jnp.int32, (1, V7X_LANES), 1) < SSD_HEAD_DIM

    def group_body(g, carry):
        cg = c_scr[g]
        bg = b_scr[g]
        cb = lax.dot_general(cg.astype(BF16), bg.astype(BF16), (((1,), (1,)), ((), ())),
                             preferred_element_type=F32)
        bg_t = bg.T
        acs_g = pltpu.roll(acs_scr[...], (V7X_LANES - SSD_HEADS_PER_GROUP * g) % V7X_LANES, axis=1)
        for q in range(SSD_PAIRS_PER_GROUP):
            m_parts, ce_parts, bw_parts, etot = [], [], [], []
            for e in range(2):
                r = 2 * q + e
                h = g * SSD_HEADS_PER_GROUP + r
                acs_col = jnp.broadcast_to(acs_g[:, r:r + 1], (L, L))
                acs_row = acst_scr[pl.ds(h, 1), :]
                dt_row = dtt_scr[pl.ds(h, 1), :]
                decay = jnp.exp(jnp.where(causal, acs_col - acs_row, -jnp.inf))
                m_parts.append(cb * decay * dt_row)
                ce_parts.append(cg * jnp.exp(acs_col))
                total = acst_scr[pl.ds(h, 1), L - 1:L]
                bw_parts.append(bg_t * (jnp.exp(total - acs_row) * dt_row))
                etot.append(jnp.exp(total))
            xq = xs_scr[g, :, q * V7X_LANES:(q + 1) * V7X_LANES]
            x_lo = jnp.where(low_half, xq, 0.0).astype(BF16)
            x_hi = jnp.where(low_half, 0.0, xq).astype(BF16)
            st = st_scr[g * SSD_PAIRS_PER_GROUP + q]
            st_lo = jnp.where(low_half, st, 0.0).astype(BF16)
            st_hi = jnp.where(low_half, 0.0, st).astype(BF16)
            lhs = jnp.concatenate(m_parts + ce_parts, axis=1).astype(BF16)
            rhs = jnp.concatenate([x_lo, x_hi, st_lo, st_hi], axis=0)
            y = jnp.dot(lhs, rhs, preferred_element_type=F32)
            y = y + dskip_ref[g, :, q * V7X_LANES:(q + 1) * V7X_LANES] * xq
            y_scr[g, :, q * V7X_LANES:(q + 1) * V7X_LANES] = y
            lhs2 = jnp.concatenate(bw_parts, axis=1).astype(BF16)
            rhs2 = jnp.concatenate([x_lo, x_hi], axis=0)
            keep = jnp.where(low_half_row, etot[0], etot[1])
            st_scr[g * SSD_PAIRS_PER_GROUP + q] = st * keep + jnp.dot(lhs2, rhs2, preferred_element_type=F32)
        return carry

    lax.fori_loop(0, SSD_GROUPS, group_body, 0)

    for g in range(SSD_GROUPS):
        sl = slice(g * gw, (g + 1) * gw)
        gated = y_scr[g] * _silu(z_ref[:, sl].astype(F32))
        ms = jnp.mean(gated * gated, axis=-1, keepdims=True)
        o_ref[:, sl] = (gated * lax.rsqrt(ms + EPS) * nw_ref[:, sl]).astype(o_ref.dtype)


def _ssd_core(wide, dt_raw, conv_w, conv_b, dt_bias, a_log, d_skip, norm_w, bsz, seqlen):
    L = SSD_CHUNK
    nc = seqlen // L
    di = SSD_D_INNER
    pad = V7X_LANES - SSD_HEADS
    row = lambda b, c: (b * nc + c, 0)
    const = lambda b, c: (0, 0)
    dskip = jnp.repeat(d_skip, SSD_HEAD_DIM).reshape(SSD_GROUPS, 1, SSD_GROUP_WIDTH)
    return pl.pallas_call(
        _ssd_kernel,
        grid=(bsz, nc),
        in_specs=[
            pl.BlockSpec((L, di), lambda b, c: (b * nc + c, 0)),
            pl.BlockSpec((L, di), lambda b, c: (b * nc + c, 1)),
            pl.BlockSpec((L, 2 * SSD_BC_WIDTH), lambda b, c: (b * nc + c, 2 * di // (2 * SSD_BC_WIDTH))),
            pl.BlockSpec((L, V7X_LANES), row),
            pl.BlockSpec((CONV_WIDTH, di), const),
            pl.BlockSpec((1, di), const),
            pl.BlockSpec((CONV_WIDTH, 2 * SSD_BC_WIDTH), const),
            pl.BlockSpec((1, 2 * SSD_BC_WIDTH), const),
            pl.BlockSpec((1, V7X_LANES), const),
            pl.BlockSpec((1, V7X_LANES), const),
            pl.BlockSpec((SSD_GROUPS, 1, SSD_GROUP_WIDTH), lambda b, c: (0, 0, 0)),
            pl.BlockSpec((1, di), const),
        ],
        out_specs=pl.BlockSpec((L, di), row),
        out_shape=jax.ShapeDtypeStruct((bsz * seqlen, di), BF16),
        scratch_shapes=[
            pltpu.VMEM((V7X_SUBLANES, di), F32),
            pltpu.VMEM((V7X_SUBLANES, 2 * SSD_BC_WIDTH), F32),
            pltpu.VMEM((SSD_GROUPS, L, SSD_GROUP_WIDTH), F32),
            pltpu.VMEM((SSD_GROUPS, L, SSD_D_STATE), F32),
            pltpu.VMEM((SSD_GROUPS, L, SSD_D_STATE), F32),
            pltpu.VMEM((SSD_GROUPS, L, SSD_GROUP_WIDTH), F32),
            pltpu.VMEM((SSD_GROUPS * SSD_PAIRS_PER_GROUP, SSD_D_STATE, V7X_LANES), F32),
            pltpu.VMEM((L, V7X_LANES), F32),
            pltpu.VMEM((V7X_LANES, L), F32),
            pltpu.VMEM((V7X_LANES, L), F32),
        ],
        compiler_params=pltpu.CompilerParams(
            dimension_semantics=("arbitrary", "arbitrary"), vmem_limit_bytes=V7X_VMEM_LIMIT_BYTES),
        name="ssd_core",
    )(wide, wide, wide, dt_raw,
      conv_w[:, :di], conv_b[:di].reshape(1, di), conv_w[:, di:], conv_b[di:].reshape(1, 2 * SSD_BC_WIDTH),
      jnp.pad(dt_bias, (0, pad)).reshape(1, V7X_LANES), jnp.pad(a_log, (0, pad)).reshape(1, V7X_LANES),
      dskip, norm_w.reshape(1, di))


def _mlstm_kernel(q_ref, k_ref, v_ref, o_ref, z_ref, gt_ref, cwq_ref, cbq_ref, cwk_ref, cbk_ref, gb_ref,
                  hnw_ref, y_ref, halo_q, halo_k, c_scr, m_scr, gates_scr):
    L = ML_CHUNK
    head = pl.program_id(1)

    @pl.when(pl.program_id(2) == 0)
    def _():
        halo_q[...] = jnp.zeros_like(halo_q)
        halo_k[...] = jnp.zeros_like(halo_k)
        c_scr[...] = jnp.zeros_like(c_scr)
        m_scr[...] = jnp.zeros_like(m_scr)

    q, tail_q = _causal_conv_silu(q_ref[...], halo_q[...], cwq_ref[...], cbq_ref[...])
    k, tail_k = _causal_conv_silu(k_ref[...], halo_k[...], cwk_ref[...], cbk_ref[...])
    halo_q[...] = tail_q
    halo_k[...] = tail_k
    k = k * (ML_QK_DIM ** -0.5)
    qb = q.astype(BF16)

    gates_scr[...] = gt_ref[...] + gb_ref[...]
    i_row = gates_scr[pl.ds(head, 1), :]
    logf_row = jax.nn.log_sigmoid(gates_scr[pl.ds(ML_HEADS + head, 1), :])
    bcum_row = _cumsum_lanes(logf_row)
    t_idx = lax.broadcasted_iota(jnp.int32, (L, L), 0)
    s_idx = lax.broadcasted_iota(jnp.int32, (L, L), 1)
    causal = s_idx <= t_idx
    bcum_col = jnp.sum(jnp.where(causal, logf_row, 0.0), axis=1, keepdims=True)
    dlog = jnp.where(causal, bcum_col - bcum_row + i_row, -jnp.inf)
    m_prev = m_scr[0:1, 0:1]
    inter = bcum_col + m_prev
    m_t = jnp.maximum(inter, jnp.max(dlog, axis=1, keepdims=True))
    qk = lax.dot_general(qb, k.astype(BF16), (((1,), (1,)), ((), ())), preferred_element_type=F32)
    s_mat = (qk * jnp.exp(dlog - m_t)).astype(BF16)
    inter_w = jnp.exp(inter - m_t)

    ones_col = (lax.broadcasted_iota(jnp.int32, (L, V7X_LANES), 1) == 0).astype(BF16)
    v_ext = jnp.concatenate([v_ref[...], ones_col], axis=1)
    c_ext = c_scr[...]
    tot = jnp.dot(s_mat, v_ext, preferred_element_type=F32)
    tot = tot + inter_w * jnp.dot(qb, c_ext.astype(BF16), preferred_element_type=F32)
    num = tot[:, :ML_V_DIM]
    den = tot[:, ML_V_DIM:ML_V_DIM + 1]
    h_tilde = num * (1.0 / jnp.maximum(jnp.abs(den), jnp.exp(-m_t)))

    total = bcum_row[:, L - 1:L]
    wlog = total - bcum_row + i_row
    m_new = jnp.maximum(total + m_prev, jnp.max(wlog, axis=1, keepdims=True))
    w_row = jnp.exp(wlog - m_new)
    carry_w = jnp.exp(total + m_prev - m_new)
    kw_t = (k.T * w_row).astype(BF16)
    c_scr[...] = carry_w * c_ext + jnp.dot(kw_t, v_ext, preferred_element_type=F32)
    m_scr[...] = jnp.broadcast_to(m_new, m_scr.shape)

    hc = _sigmoid(o_ref[...].astype(F32)) * h_tilde
    ms = jnp.mean(hc * hc, axis=-1, keepdims=True)
    hc = hc * lax.rsqrt(ms + EPS) * hnw_ref[...]
    y_ref[...] = (hc * _silu(z_ref[...].astype(F32))).astype(y_ref.dtype)


def _mlstm_core(wide, gates_t, conv_w, conv_b, igate_b, fgate_b, head_norm_w, bsz, seqlen):
    L = ML_CHUNK
    nt = seqlen // L
    kd, vd, nh = ML_QK_DIM, ML_V_DIM, ML_HEADS
    k_off = nh * kd // kd
    v_off = ML_QK_WIDTH // vd
    o_off = (ML_QK_WIDTH + ML_D_INNER) // vd
    z_off = (ML_QK_WIDTH + 2 * ML_D_INNER) // vd
    gate_bias = jnp.zeros((V7X_LANES, 1), F32).at[:nh, 0].set(igate_b).at[nh:2 * nh, 0].set(fgate_b)
    return pl.pallas_call(
        _mlstm_kernel,
        grid=(bsz, nh, nt),
        in_specs=[
            pl.BlockSpec((L, kd), lambda b, h, t: (b * nt + t, h)),
            pl.BlockSpec((L, kd), lambda b, h, t: (b * nt + t, k_off + h)),
            pl.BlockSpec((L, vd), lambda b, h, t: (b * nt + t, v_off + h)),
            pl.BlockSpec((L, vd), lambda b, h, t: (b * nt + t, o_off + h)),
            pl.BlockSpec((L, vd), lambda b, h, t: (b * nt + t, z_off + h)),
            pl.BlockSpec((V7X_LANES, L), lambda b, h, t: (0, b * nt + t)),
            pl.BlockSpec((CONV_WIDTH, kd), lambda b, h, t: (0, h)),
            pl.BlockSpec((1, kd), lambda b, h, t: (0, h)),
            pl.BlockSpec((CONV_WIDTH, kd), lambda b, h, t: (0, k_off + h)),
            pl.BlockSpec((1, kd), lambda b, h, t: (0, k_off + h)),
            pl.BlockSpec((V7X_LANES, 1), lambda b, h, t: (0, 0)),
            pl.BlockSpec((1, vd), lambda b, h, t: (0, h)),
        ],
        out_specs=pl.BlockSpec((L, vd), lambda b, h, t: (b * nt + t, h)),
        out_shape=jax.ShapeDtypeStruct((bsz * seqlen, ML_D_INNER), BF16),
        scratch_shapes=[
            pltpu.VMEM((V7X_SUBLANES, kd), F32),
            pltpu.VMEM((V7X_SUBLANES, kd), F32),
            pltpu.VMEM((kd, ML_VEXT), F32),
            pltpu.VMEM((V7X_SUBLANES, V7X_LANES), F32),
            pltpu.VMEM((V7X_LANES, L), F32),
        ],
        compiler_params=pltpu.CompilerParams(
            dimension_semantics=("arbitrary", "arbitrary", "arbitrary"), vmem_limit_bytes=V7X_VMEM_LIMIT_BYTES),
        name="mlstm_core",
    )(wide, wide, wide, wide, wide, gates_t, conv_w, conv_b.reshape(1, ML_QK_WIDTH), conv_w,
      conv_b.reshape(1, ML_QK_WIDTH), gate_bias, head_norm_w.reshape(1, ML_D_INNER))


def _out_kernel(g_ref, w_ref, x_ref, gate_ref, fnw_ref, o_ref, acc_ref, *, final_norm):
    kk = pl.program_id(1)

    @pl.when(kk == 0)
    def _():
        acc_ref[...] = jnp.zeros_like(acc_ref)

    acc_ref[...] += jnp.dot(g_ref[...], w_ref[...], preferred_element_type=F32)

    @pl.when(kk == pl.num_programs(1) - 1)
    def _():
        y = x_ref[...] + gate_ref[0] * acc_ref[...]
        if final_norm:
            ms = jnp.mean(y * y, axis=-1, keepdims=True)
            y = y * lax.rsqrt(ms + EPS) * fnw_ref[...]
        o_ref[...] = y


def _out_proj_residual(g, w_out, x2, gate, final_norm_w, seqlen, *, final_norm):
    bt, kdim = g.shape
    d = w_out.shape[1]
    tm, tk = min(512, seqlen), 1024
    tiles_per_seq = seqlen // tm
    return pl.pallas_call(
        functools.partial(_out_kernel, final_norm=final_norm),
        grid=(bt // tm, kdim // tk),
        in_specs=[
            pl.BlockSpec((tm, tk), lambda i, k: (i, k)),
            pl.BlockSpec((tk, d), lambda i, k: (k, 0)),
            pl.BlockSpec((tm, d), lambda i, k: (i, 0)),
            pl.BlockSpec((1, 1, d), lambda i, k: (i // tiles_per_seq, 0, 0)),
            pl.BlockSpec((1, d), lambda i, k: (0, 0)),
        ],
        out_specs=pl.BlockSpec((tm, d), lambda i, k: (i, 0)),
        out_shape=jax.ShapeDtypeStruct((bt, d), F32),
        scratch_shapes=[pltpu.VMEM((tm, d), F32)],
        compiler_params=pltpu.CompilerParams(
            dimension_semantics=("arbitrary", "arbitrary"), vmem_limit_bytes=V7X_VMEM_LIMIT_BYTES),
        name="out_proj_residual",
    )(g, w_out, x2, gate, final_norm_w.reshape(1, d))


def kernel(x, c, norm_w, ada_w, ada_b, ssd_w_in, ssd_conv_w, ssd_conv_b, ssd_dt_bias, ssd_a_log, ssd_d,
           ssd_norm_w, ssd_w_out, ml_w_in, ml_conv_w, ml_conv_b, ml_igate_b, ml_fgate_b, ml_norm_w, ml_w_out,
           final_norm_w):
    bsz, seqlen, d = x.shape
    x2 = x.reshape(bsz * seqlen, d)
    mod = _ada_modulation(c, ada_w, ada_b)
    shift, scale, gate = (mod[:, :, i * d:(i + 1) * d].reshape(mod.shape[0], bsz, 1, d) for i in range(3))

    w_in = ssd_w_in[0]
    w_dt = jnp.pad(w_in[:, SSD_WIDE:], ((0, 0), (0, V7X_LANES - SSD_HEADS))).astype(BF16)
    wide, dt_raw = _norm_mod_proj(x2, norm_w[0], scale[0], shift[0], w_in[:, :SSD_WIDE].astype(BF16), w_dt,
                                  seqlen, small_transposed=False)
    g = _ssd_core(wide, dt_raw, ssd_conv_w[0], ssd_conv_b[0], ssd_dt_bias[0], ssd_a_log[0], ssd_d[0],
                  ssd_norm_w[0], bsz, seqlen)
    x2 = _out_proj_residual(g, ssd_w_out[0].astype(BF16), x2, gate[0], final_norm_w, seqlen, final_norm=False)

    w_in = ml_w_in[0]
    w_gates = jnp.pad(w_in[:, ML_WIDE:], ((0, 0), (0, V7X_LANES - 2 * ML_HEADS))).astype(BF16).T
    wide, gates_t = _norm_mod_proj(x2, norm_w[1], scale[1], shift[1], w_in[:, :ML_WIDE].astype(BF16), w_gates,
                                   seqlen, small_transposed=True)
    y = _mlstm_core(wide, gates_t, ml_conv_w[0], ml_conv_b[0], ml_igate_b[0], ml_fgate_b[0], ml_norm_w[0],
                    bsz, seqlen)
    out = _out_proj_residual(y, ml_w_out[0].astype(BF16), x2, gate[1], final_norm_w, seqlen, final_norm=True)
    return out.reshape(bsz, seqlen, d)
```
